```python
import jax, jax.numpy as jnp
from jax import lax
import numpy as np

D_MODEL = 2048
BATCH = 4
SEQ = 2048
DEPTH = 1
DEC_BATCH = 32
DEC_SEQ = 4
PAST_LEN = 8192
PAGE_SIZE = 128

MLSTM_HEADS = 8
MLSTM_DH = 128
MLSTM_W = MLSTM_HEADS * MLSTM_DH
MLSTM_CHUNK = 64
MOBA_HEADS = 8
MOBA_DH = 128
MOBA_W = MOBA_HEADS * MOBA_DH
MOBA_BLOCK = 256
MOBA_TOPK = 3
MOBA_QCHUNK = 32
N_MEM = 256
XATTN_HEADS = 4
XATTN_DH = 128
XATTN_W = XATTN_HEADS * XATTN_DH
D_FF = -(-8 * D_MODEL // (3 * 256)) * 256
NORM_EPS = 1e-6
IN_SPLITS = (MLSTM_W, MLSTM_W, MLSTM_W, MLSTM_W, 2 * MLSTM_HEADS, MOBA_W, MOBA_W, MOBA_W, D_MODEL, D_MODEL)
IN_WIDTH = sum(IN_SPLITS)
IN_OFFSETS = tuple(sum(IN_SPLITS[:i + 1]) for i in range(len(IN_SPLITS) - 1))

kernel_name = 'hybrid_mlstm_moba_xattn_decode_step'


def rms_norm(x, g):
    xf = x.astype(jnp.float32)
    y = xf * lax.rsqrt(jnp.mean(xf * xf, axis=-1, keepdims=True) + NORM_EPS)
    return (y * g.astype(jnp.float32)).astype(x.dtype)


def alibi_slopes(n):
    return jnp.exp2(-8.0 * jnp.arange(1, n + 1, dtype=jnp.float32) / n)


def mlstm_chunkwise(q, k, v, ig, lf, C0, n0, m0):
    B, T, H, DH = q.shape
    L = MLSTM_CHUNK if T % MLSTM_CHUNK == 0 else T
    nc = T // L

    def chunks(a):
        return jnp.moveaxis(a.reshape((B, nc, L) + a.shape[2:]), 1, 0)

    causal = jnp.tril(jnp.ones((L, L), dtype=bool))[None, :, :, None]

    def step(carry, xs):
        C, n, m = carry
        qc, kc, vc, igc, lfc = xs
        b = jnp.cumsum(lfc, axis=1)
        logw = jnp.where(causal, b[:, :, None, :] - b[:, None, :, :] + igc[:, None, :, :], -jnp.inf)
        log_prev = b + m[:, None, :]
        m_t = jnp.maximum(log_prev, jnp.max(logw, axis=2))
        w = jnp.exp(logw - m_t[:, :, None, :])
        w_prev = jnp.exp(log_prev - m_t)
        s = jnp.einsum('bjhd,bshd->bjsh', qc, kc) * w
        num = jnp.einsum('bjsh,bshe->bjhe', s, vc) + w_prev[..., None] * jnp.einsum('bjhd,bhde->bjhe', qc, C)
        den = jnp.sum(s, axis=2) + w_prev * jnp.einsum('bjhd,bhd->bjh', qc, n)
        h = num / jnp.maximum(jnp.abs(den), jnp.exp(-m_t))[..., None]
        m_end = m_t[:, -1]
        w_end = jnp.exp(b[:, -1:] - b + igc - m_end[:, None])
        decay = jnp.exp(b[:, -1] + m - m_end)
        C = decay[..., None, None] * C + jnp.einsum('bsh,bshd,bshe->bhde', w_end, kc, vc)
        n = decay[..., None] * n + jnp.einsum('bsh,bshd->bhd', w_end, kc)
        return (C, n, m_end), h

    (C, n, m), hs = lax.scan(step, (C0, n0, m0), (chunks(q), chunks(k), chunks(v), chunks(ig), chunks(lf)))
    return jnp.moveaxis(hs, 0, 1).reshape(B, T, H, DH), C, n, m


def moba_attention(q, pos, k_all, v_all):
    B, T, H, DH = q.shape
    NB = k_all.shape[1] // MOBA_BLOCK
    kb = k_all.reshape(B, NB, MOBA_BLOCK, H, DH)
    vb = v_all.reshape(B, NB, MOBA_BLOCK, H, DH)
    kbar = jnp.mean(kb, axis=2, dtype=jnp.float32)
    topk = min(MOBA_TOPK, NB)
    slopes = alibi_slopes(H)
    qb = MOBA_QCHUNK if T % MOBA_QCHUNK == 0 else T
    nq = T // qb
    q_chunks = jnp.moveaxis(q.reshape(B, nq, qb, H, DH), 1, 0)
    pos_chunks = pos.reshape(nq, qb)
    b_idx = jnp.arange(B)[:, None, None, None]
    h_idx = jnp.arange(H)[None, None, :, None]
    blk_ids = jnp.arange(NB)
    offs = jnp.arange(MOBA_BLOCK)

    def attend(args):
        qq, pp = args
        own = pp // MOBA_BLOCK
        gate = jnp.einsum('bthd,bnhd->bthn', qq.astype(jnp.float32), kbar)
        past_blk = (blk_ids[None, :] < own[:, None])[None, :, None, :]
        gate = jnp.where(past_blk, gate, -jnp.inf)
        _, top = lax.top_k(gate, topk)
        own_b = jnp.broadcast_to(own[None, :, None, None], (B, qb, H, 1)).astype(top.dtype)
        sel = jnp.concatenate([top, own_b], axis=-1)
        blk_ok = jnp.concatenate([top < own_b, jnp.ones_like(own_b, dtype=bool)], axis=-1)
        kg = kb[b_idx, sel, :, h_idx]
        vg = vb[b_idx, sel, :, h_idx]
        dist = pp[None, :, None, None, None] - (sel[..., None] * MOBA_BLOCK + offs)
        ok = blk_ok[..., None] & (dist >= 0)
        s = (jnp.einsum('bthd,bthnkd->bthnk', qq, kg).astype(jnp.float32)
             - slopes[None, None, :, None, None] * dist.astype(jnp.float32))
        s = jnp.where(ok, s, -jnp.inf)
        p = jax.nn.softmax(s.reshape(B, qb, H, -1), axis=-1).reshape(s.shape)
        return jnp.einsum('bthnk,bthnkd->bthd', p.astype(vg.dtype), vg).astype(q.dtype)

    out = lax.map(attend, (q_chunks, pos_chunks))
    return jnp.moveaxis(out, 0, 1).reshape(B, T, H, DH)


def memory_kv(mem, norm_mem, w_xk, w_xv):
    B, M, _ = mem.shape
    mn = rms_norm(mem, norm_mem)
    return ((mn @ w_xk).reshape(B, M, XATTN_HEADS, XATTN_DH),
            (mn @ w_xv).reshape(B, M, XATTN_HEADS, XATTN_DH))


def cross_attend(xn, mem_k, mem_v, w_xq, w_xo):
    B, T, _ = xn.shape
    q = (xn @ w_xq).reshape(B, T, XATTN_HEADS, XATTN_DH) * (XATTN_DH ** -0.5)
    s = jnp.einsum('bthd,bmhd->bhtm', q, mem_k).astype(jnp.float32)
    p = jax.nn.softmax(s, axis=-1).astype(mem_v.dtype)
    o = jnp.einsum('bhtm,bmhd->bthd', p, mem_v).reshape(B, T, XATTN_W).astype(xn.dtype)
    return o @ w_xo


def decoder_layer(x, pos, C0, n0, m0, past_k, past_v, mem_k, mem_v, W):
    B, T, _ = x.shape
    f32 = jnp.float32
    xn = rms_norm(x, W['norm_mix'])
    q_m, k_m, v_m, o_m, if_pre, q_a, k_a, v_a, g_m, g_a = jnp.split(xn @ W['w_in'], IN_OFFSETS, axis=-1)
    gates = if_pre.astype(f32) + W['b_if'].astype(f32)
    ig = gates[..., :MLSTM_HEADS]
    lf = jax.nn.log_sigmoid(gates[..., MLSTM_HEADS:])
    hm = lambda a: a.reshape(B, T, MLSTM_HEADS, MLSTM_DH).astype(f32)
    h_m, C, n, m = mlstm_chunkwise(hm(q_m), hm(k_m) * (MLSTM_DH ** -0.5), hm(v_m), ig, lf,
                                   C0.astype(f32), n0.astype(f32), m0.astype(f32))
    h_m = (h_m * lax.rsqrt(jnp.mean(h_m * h_m, axis=-1, keepdims=True) + NORM_EPS)
           * W['mh_gain'].astype(f32).reshape(MLSTM_HEADS, MLSTM_DH))
    h_m = (h_m.reshape(B, T, MLSTM_W) * jax.nn.sigmoid(o_m.astype(f32))).astype(x.dtype)
    ha = lambda a: a.reshape(B, T, MOBA_HEADS, MOBA_DH)
    k_a, v_a = ha(k_a), ha(v_a)
    if past_k is None:
        k_all, v_all = k_a, v_a
    else:
        k_all = jnp.concatenate([past_k, k_a.astype(past_k.dtype)], axis=1)
        v_all = jnp.concatenate([past_v, v_a.astype(past_v.dtype)], axis=1)
    pad = (-k_all.shape[1]) % MOBA_BLOCK
    if pad:
        k_all = jnp.pad(k_all, ((0, 0), (0, pad), (0, 0), (0, 0)))
        v_all = jnp.pad(v_all, ((0, 0), (0, pad), (0, 0), (0, 0)))
    h_a = moba_attention(ha(q_a) * (MOBA_DH ** -0.5), pos, k_all, v_all).reshape(B, T, MOBA_W)
    u = jax.nn.sigmoid(g_m) * (h_m @ W['w_proj_m']) + jax.nn.sigmoid(g_a) * (h_a @ W['w_proj_a'])
    x = x + u @ W['w_out']
    x = x + cross_attend(rms_norm(x, W['norm_xattn']), mem_k, mem_v, W['w_xq'], W['w_xo'])
    xf = rms_norm(x, W['norm_ffn'])
    x = x + (jax.nn.silu(xf @ W['w_ffn_gate']) * (xf @ W['w_ffn_up'])) @ W['w_ffn_down']
    return x, k_a, v_a, C, n, m


def setup_inputs(seed: int = 0) -> dict:
    key = jax.random.key(seed)
    ks = iter(jax.random.split(key, 48))
    n_pages = PAST_LEN // PAGE_SIZE
    n_used = DEC_BATCH * n_pages
    n_phys = n_used + max(1, n_used // 4)
    D, L = D_MODEL, DEPTH

    def nrm(shape, scale=1.0):
        return scale * jax.random.normal(next(ks), shape, jnp.float32)

    def gain(n):
        return 1.0 + nrm((L, n), 0.01)

    page_table = jax.random.permutation(next(ks), n_phys)[:n_used].reshape(DEC_BATCH, n_pages).astype(jnp.int32)
    b_if = jnp.concatenate([nrm((L, MLSTM_HEADS), 0.1),
                            jnp.linspace(3.0, 6.0, MLSTM_HEADS, dtype=jnp.float32)[None] + nrm((L, MLSTM_HEADS), 0.1)], axis=-1)
    return {
        'x_prompt': nrm((BATCH, SEQ, D)),
        'x_sample': nrm((DEC_BATCH, DEC_SEQ, D)),
        'cache_k': nrm((L, n_phys, PAGE_SIZE, MOBA_HEADS, MOBA_DH)),
        'cache_v': nrm((L, n_phys, PAGE_SIZE, MOBA_HEADS, MOBA_DH)),
        'cache_mem_k': nrm((L, DEC_BATCH, N_MEM, XATTN_HEADS, XATTN_DH)),
        'cache_mem_v': nrm((L, DEC_BATCH, N_MEM, XATTN_HEADS, XATTN_DH)),
        'state_C': nrm((L, DEC_BATCH, MLSTM_HEADS, MLSTM_DH, MLSTM_DH), 0.5),
        'state_n': 0.5 * jnp.abs(nrm((L, DEC_BATCH, MLSTM_HEADS, MLSTM_DH))),
        'state_m': nrm((L, DEC_BATCH, MLSTM_HEADS)),
        'page_table': page_table,
        'mem_prompt': nrm((BATCH, N_MEM, D)),
        'norm_mix': gain(D),
        'w_in': nrm((L, D, IN_WIDTH), D ** -0.5),
        'b_if': b_if,
        'mh_gain': gain(MLSTM_W),
        'w_proj_m': nrm((L, MLSTM_W, D), MLSTM_W ** -0.5),
        'w_proj_a': nrm((L, MOBA_W, D), MOBA_W ** -0.5),
        'w_out': nrm((L, D, D), D ** -0.5),
        'norm_xattn': gain(D),
        'norm_mem': gain(D),
        'w_xq': nrm((L, D, XATTN_W), D ** -0.5),
        'w_xk': nrm((L, D, XATTN_W), D ** -0.5),
        'w_xv': nrm((L, D, XATTN_W), D ** -0.5),
        'w_xo': nrm((L, XATTN_W, D), XATTN_W ** -0.5),
        'norm_ffn': gain(D),
        'w_ffn_gate': nrm((L, D, D_FF), D ** -0.5),
        'w_ffn_up': nrm((L, D, D_FF), D ** -0.5),
        'w_ffn_down': nrm((L, D_FF, D), D_FF ** -0.5),
        'norm_final': 1.0 + nrm((D,), 0.01),
    }


def reference(x_prompt, x_sample, cache_k, cache_v, cache_mem_k, cache_mem_v, state_C, state_n, state_m,
              page_table, mem_prompt, norm_mix, w_in, b_if, mh_gain, w_proj_m, w_proj_a, w_out,
              norm_xattn, norm_mem, w_xq, w_xk, w_xv, w_xo, norm_ffn, w_ffn_gate, w_ffn_up, w_ffn_down,
              norm_final):
    B, T, _ = x_prompt.shape
    DB, TS, _ = x_sample.shape
    past = page_table.shape[1] * cache_k.shape[2]
    pos_p = jnp.arange(T, dtype=jnp.int32)
    pos_s = past + jnp.arange(TS, dtype=jnp.int32)
    hp, hs = x_prompt, x_sample
    kp_l, vp_l, Cp_l, np_l, mp_l, mkp_l, mvp_l = [], [], [], [], [], [], []
    ks_l, vs_l, Cs_l, ns_l, ms_l = [], [], [], [], []
    for l in range(DEPTH):
        W = dict(norm_mix=norm_mix[l], w_in=w_in[l], b_if=b_if[l], mh_gain=mh_gain[l],
                 w_proj_m=w_proj_m[l], w_proj_a=w_proj_a[l], w_out=w_out[l],
                 norm_xattn=norm_xattn[l], w_xq=w_xq[l], w_xo=w_xo[l],
                 norm_ffn=norm_ffn[l], w_ffn_gate=w_ffn_gate[l], w_ffn_up=w_ffn_up[l],
                 w_ffn_down=w_ffn_down[l])
        mk_p, mv_p = memory_kv(mem_prompt, norm_mem[l], w_xk[l], w_xv[l])
        C0 = jnp.zeros((B, MLSTM_HEADS, MLSTM_DH, MLSTM_DH), jnp.float32)
        n0 = jnp.zeros((B, MLSTM_HEADS, MLSTM_DH), jnp.float32)
        m0 = jnp.zeros((B, MLSTM_HEADS), jnp.float32)
        hp, kp, vp, Cp, n_p, mp = decoder_layer(hp, pos_p, C0, n0, m0, None, None, mk_p, mv_p, W)
        past_k = cache_k[l][page_table].reshape(DB, past, MOBA_HEADS, MOBA_DH)
        past_v = cache_v[l][page_table].reshape(DB, past, MOBA_HEADS, MOBA_DH)
        hs, k_s, v_s, Cs, n_s, ms = decoder_layer(hs, pos_s, state_C[l], state_n[l], state_m[l],
                                                  past_k, past_v, cache_mem_k[l], cache_mem_v[l], W)
        kp_l.append(kp); vp_l.append(vp); Cp_l.append(Cp); np_l.append(n_p); mp_l.append(mp)
        mkp_l.append(mk_p); mvp_l.append(mv_p)
        ks_l.append(k_s); vs_l.append(v_s); Cs_l.append(Cs); ns_l.append(n_s); ms_l.append(ms)
    y_prompt = rms_norm(hp, norm_final)
    y_sample = rms_norm(hs, norm_final)
    k_prompt = jnp.stack(kp_l)
    v_prompt = jnp.stack(vp_l)
    C_prompt = jnp.stack(Cp_l).astype(state_C.dtype)
    n_prompt = jnp.stack(np_l).astype(state_n.dtype)
    m_prompt = jnp.stack(mp_l).astype(state_m.dtype)
    mem_k_prompt = jnp.stack(mkp_l)
    mem_v_prompt = jnp.stack(mvp_l)
    k_sample = jnp.stack(ks_l).astype(cache_k.dtype)
    v_sample = jnp.stack(vs_l).astype(cache_v.dtype)
    C_sample = jnp.stack(Cs_l).astype(state_C.dtype)
    n_sample = jnp.stack(ns_l).astype(state_n.dtype)
    m_sample = jnp.stack(ms_l).astype(state_m.dtype)
    return (y_prompt, y_sample, k_prompt, v_prompt, C_prompt, n_prompt, m_prompt, mem_k_prompt, mem_v_prompt,
            k_sample, v_sample, C_sample, n_sample, m_sample)
```

```python
import functools

import jax
import jax.numpy as jnp
from jax import lax
from jax.experimental import pallas as pl
from jax.experimental.pallas import tpu as pltpu

F32 = jnp.float32
BF16 = jnp.bfloat16

D_MODEL = 2048
N_HEADS = 8
D_HEAD = 128
MIX_W = N_HEADS * D_HEAD
MOBA_BLOCK = 256
MOBA_TOPK = 3
PAGE_SIZE = 128
PAGES_PER_BLOCK = MOBA_BLOCK // PAGE_SIZE
XATTN_HEADS = 4
XATTN_W = XATTN_HEADS * D_HEAD
N_MEM = 256
NORM_EPS = 1e-6
LANES = 128
NEG_INF = float("-inf")
VMEM_LIMIT = 52 * 1024 * 1024

HIGHEST = lax.Precision.HIGHEST


def _params(n_axes, vmem=VMEM_LIMIT):
    return pltpu.CompilerParams(dimension_semantics=("arbitrary",) * n_axes, vmem_limit_bytes=vmem)


def _dot(a, b, precision=None):
    return jnp.dot(a, b, preferred_element_type=F32, precision=precision)


def _dot_nt(a, b, precision=None):
    return lax.dot_general(a, b, (((1,), (1,)), ((), ())), preferred_element_type=F32, precision=precision)


def _dot_tn(a, b, precision=None):
    return lax.dot_general(a, b, (((0,), (0,)), ((), ())), preferred_element_type=F32, precision=precision)


def _rms(x, g):
    return x * lax.rsqrt(jnp.mean(x * x, axis=-1, keepdims=True) + NORM_EPS) * g


def _sigmoid(x):
    return 1.0 / (1.0 + jnp.exp(-x))


def _rmsnorm_kernel(x_ref, g_ref, o_ref):
    o_ref[...] = _rms(x_ref[...].astype(F32), g_ref[...]).astype(o_ref.dtype)


def rmsnorm(x, g, tm, out_dtype=BF16):
    m, d = x.shape
    return pl.pallas_call(
        _rmsnorm_kernel,
        grid=(m // tm,),
        in_specs=[pl.BlockSpec((tm, d), lambda i: (i, 0)), pl.BlockSpec((1, d), lambda i: (0, 0))],
        out_specs=pl.BlockSpec((tm, d), lambda i: (i, 0)),
        out_shape=jax.ShapeDtypeStruct((m, d), out_dtype),
        compiler_params=_params(1),
        name="rmsnorm",
    )(x, g.reshape(1, d).astype(F32))


def _mm_kernel(a_ref, w_ref, s_ref, o_ref):
    acc = _dot(a_ref[...], w_ref[...])
    o_ref[...] = (acc * s_ref[...]).astype(o_ref.dtype)


def matmul(a, w, col_off, n_cols, tm, tn, out_dtype, col_scale=None, name="matmul"):
    m, k = a.shape
    assert col_off % tn == 0 and n_cols % tn == 0 and m % tm == 0
    off = col_off // tn
    if col_scale is None:
        col_scale = jnp.ones((n_cols,), F32)
    return pl.pallas_call(
        _mm_kernel,
        grid=(m // tm, n_cols // tn),
        in_specs=[pl.BlockSpec((tm, k), lambda i, j: (i, 0)),
                  pl.BlockSpec((k, tn), lambda i, j: (0, j + off)),
                  pl.BlockSpec((1, tn), lambda i, j: (0, j))],
        out_specs=pl.BlockSpec((tm, tn), lambda i, j: (i, j)),
        out_shape=jax.ShapeDtypeStruct((m, n_cols), out_dtype),
        compiler_params=_params(2),
        name=name,
    )(a, w, col_scale.reshape(1, n_cols).astype(F32))


def _gates_kernel(a_ref, w_ref, b_ref, o_ref):
    o_ref[...] = _dot(a_ref[...].astype(F32), w_ref[...], precision=HIGHEST) + b_ref[...]


def gate_preacts(xn, w_if, b_if, tm):
    m, k = xn.shape
    n = w_if.shape[1]
    w_pad = jnp.pad(w_if.astype(F32), ((0, 0), (0, LANES - n)))
    b_pad = jnp.pad(b_if.astype(F32), (0, LANES - n)).reshape(1, LANES)
    return pl.pallas_call(
        _gates_kernel,
        grid=(m // tm,),
        in_specs=[pl.BlockSpec((tm, k), lambda i: (i, 0)),
                  pl.BlockSpec((k, LANES), lambda i: (0, 0)),
                  pl.BlockSpec((1, LANES), lambda i: (0, 0))],
        out_specs=pl.BlockSpec((tm, LANES), lambda i: (i, 0)),
        out_shape=jax.ShapeDtypeStruct((m, LANES), F32),
        compiler_params=_params(1),
        name="gate_preacts",
    )(xn, w_pad, b_pad)


def _proj_kernel(hm_ref, ha_ref, wm_ref, wa_ref, gm_ref, ga_ref, o_ref):
    pm = _dot(hm_ref[...], wm_ref[...])
    pa = _dot(ha_ref[...], wa_ref[...])
    u = _sigmoid(gm_ref[...].astype(F32)) * pm + _sigmoid(ga_ref[...].astype(F32)) * pa
    o_ref[...] = u.astype(o_ref.dtype)


def gated_merge(hm, ha, wm, wa, g, tm, tn):
    m, k = hm.shape
    n = wm.shape[1]
    nb = n // tn
    return pl.pallas_call(
        _proj_kernel,
        grid=(m // tm, nb),
        in_specs=[pl.BlockSpec((tm, k), lambda i, j: (i, 0)),
                  pl.BlockSpec((tm, k), lambda i, j: (i, 0)),
                  pl.BlockSpec((k, tn), lambda i, j: (0, j)),
                  pl.BlockSpec((k, tn), lambda i, j: (0, j)),
                  pl.BlockSpec((tm, tn), lambda i, j: (i, j)),
                  pl.BlockSpec((tm, tn), lambda i, j: (i, j + nb))],
        out_specs=pl.BlockSpec((tm, tn), lambda i, j: (i, j)),
        out_shape=jax.ShapeDtypeStruct((m, n), BF16),
        compiler_params=_params(2),
        name="gated_merge",
    )(hm, ha, wm, wa, g, g)


def _resid_mm_norm_kernel(x_ref, a_ref, w_ref, g_ref, xo_ref, xn_ref):
    x1 = x_ref[...] + _dot(a_ref[...], w_ref[...])
    xo_ref[...] = x1
    xn_ref[...] = _rms(x1, g_ref[...]).astype(xn_ref.dtype)


def resid_matmul_norm(x, a, w, g, tm, name):
    m, d = x.shape
    k = a.shape[1]
    return pl.pallas_call(
        _resid_mm_norm_kernel,
        grid=(m // tm,),
        in_specs=[pl.BlockSpec((tm, d), lambda i: (i, 0)),
                  pl.BlockSpec((tm, k), lambda i: (i, 0)),
                  pl.BlockSpec((k, d), lambda i: (0, 0)),
                  pl.BlockSpec((1, d), lambda i: (0, 0))],
        out_specs=[pl.BlockSpec((tm, d), lambda i: (i, 0)), pl.BlockSpec((tm, d), lambda i: (i, 0))],
        out_shape=[jax.ShapeDtypeStruct((m, d), F32), jax.ShapeDtypeStruct((m, d), BF16)],
        compiler_params=_params(1),
        name=name,
    )(x, a, w, g.reshape(1, d).astype(F32))


def _ffn_kernel(xf_ref, x_ref, wg_ref, wu_ref, wd_ref, gf_ref, y_ref, acc_ref):
    k = pl.program_id(1)

    @pl.when(k == 0)
    def _():
        acc_ref[...] = jnp.zeros_like(acc_ref)

    xf = xf_ref[...]
    g = _dot(xf, wg_ref[...])
    u = _dot(xf, wu_ref[...])
    hidden = (g * _sigmoid(g)) * u
    acc_ref[...] += _dot(hidden.astype(BF16), wd_ref[...])

    @pl.when(k == pl.num_programs(1) - 1)
    def _():
        y_ref[...] = _rms(x_ref[...] + acc_ref[...], gf_ref[...])


def ffn_final_norm(xf, x, wg, wu, wd, g_final, tm, tf):
    m, d = x.shape
    f = wg.shape[1]
    return pl.pallas_call(
        _ffn_kernel,
        grid=(m // tm, f // tf),
        in_specs=[pl.BlockSpec((tm, d), lambda i, k: (i, 0)),
                  pl.BlockSpec((tm, d), lambda i, k: (i, 0)),
                  pl.BlockSpec((d, tf), lambda i, k: (0, k)),
                  pl.BlockSpec((d, tf), lambda i, k: (0, k)),
                  pl.BlockSpec((tf, d), lambda i, k: (k, 0)),
                  pl.BlockSpec((1, d), lambda i, k: (0, 0))],
        out_specs=pl.BlockSpec((tm, d), lambda i, k: (i, 0)),
        out_shape=jax.ShapeDtypeStruct((m, d), F32),
        scratch_shapes=[pltpu.VMEM((tm, d), F32)],
        compiler_params=_params(2),
        name="ffn_final_norm",
    )(xf, x, wg, wu, wd, g_final.reshape(1, d).astype(F32))


def _xattn_kernel(q_ref, k_ref, v_ref, o_ref, *, mxu_dtype):
    q = q_ref[...]
    for h in range(XATTN_HEADS):
        sl = slice(h * D_HEAD, (h + 1) * D_HEAD)
        s = _dot_nt(q[:, sl].astype(mxu_dtype), k_ref[:, sl].astype(mxu_dtype))
        p = jnp.exp(s - jnp.max(s, axis=-1, keepdims=True))
        o = _dot(p.astype(mxu_dtype), v_ref[:, sl].astype(mxu_dtype))
        o_ref[:, sl] = (o / jnp.sum(p, axis=-1, keepdims=True)).astype(o_ref.dtype)


def cross_attention(q, mem_k, mem_v, tq):
    b, t, w = q.shape
    mxu_dtype = BF16 if tq >= 16 else F32
    return pl.pallas_call(
        functools.partial(_xattn_kernel, mxu_dtype=mxu_dtype),
        grid=(b, t // tq),
        in_specs=[pl.BlockSpec((None, tq, w), lambda i, j: (i, j, 0)),
                  pl.BlockSpec((None, N_MEM, w), lambda i, j: (i, 0, 0)),
                  pl.BlockSpec((None, N_MEM, w), lambda i, j: (i, 0, 0))],
        out_specs=pl.BlockSpec((None, tq, w), lambda i, j: (i, j, 0)),
        out_shape=jax.ShapeDtypeStruct((b, t, w), q.dtype),
        compiler_params=_params(2),
        name="cross_attention",
    )(q, mem_k, mem_v)


def _mlstm_kernel(*refs, chunk, valid, has_init, mxu_dtype):
    if has_init:
        qkvo_ref, g_ref, gain_ref, c0_ref, n0_ref, m0_ref, h_ref, co_ref, no_ref, mo_ref, c_s, n_s, m_s = refs
    else:
        qkvo_ref, g_ref, gain_ref, h_ref, co_ref, no_ref, mo_ref, c_s, n_s, m_s = refs
    c = pl.program_id(1)
    L = chunk

    @pl.when(c == 0)
    def _():
        if has_init:
            c_s[...] = c0_ref[...]
            n_s[...] = n0_ref[...]
            m_s[...] = m0_ref[...]
        else:
            c_s[...] = jnp.zeros_like(c_s)
            n_s[...] = jnp.zeros_like(n_s)
            m_s[...] = jnp.zeros_like(m_s)

    g = g_ref[...]
    row = lax.broadcasted_iota(jnp.int32, (L, LANES), 0)
    lane = lax.broadcasted_iota(jnp.int32, (L, LANES), 1)
    log_f = jnp.minimum(g, 0.0) - jnp.log(1.0 + jnp.exp(-jnp.abs(g)))
    is_f = (lane >= N_HEADS) & (lane < 2 * N_HEADS) & (row < valid)
    log_f = jnp.where(is_f, log_f, 0.0)
    rr = lax.broadcasted_iota(jnp.int32, (L, L), 0)
    cc = lax.broadcasted_iota(jnp.int32, (L, L), 1)
    tril = (cc <= rr).astype(F32)
    b_all = _dot(tril, log_f, precision=HIGHEST)
    ig_shift = pltpu.roll(g, N_HEADS, axis=1)
    d_all = jnp.where((lane >= N_HEADS) & (lane < 2 * N_HEADS), b_all - ig_shift, 0.0)
    keep = (cc <= rr) & (cc < valid)
    src_ok = lax.broadcasted_iota(jnp.int32, (L, 1), 0) < valid
    gain = gain_ref[...]

    for h in range(N_HEADS):
        col = N_HEADS + h
        sel = (lane == col).astype(F32)
        d_row = _dot_nt(sel, d_all, precision=HIGHEST)
        b_col = b_all[:, col:col + 1]
        ig_col = g[:, h:h + 1]
        m_prev = m_s[h:h + 1, 0:1]
        logw = jnp.where(keep, b_col - d_row, NEG_INF)
        log_prev = b_col + m_prev
        m_t = jnp.maximum(log_prev, jnp.max(logw, axis=1, keepdims=True))
        w = jnp.exp(logw - m_t)
        w_prev = jnp.exp(log_prev - m_t)

        q = qkvo_ref[:, h * D_HEAD:(h + 1) * D_HEAD]
        k = qkvo_ref[:, MIX_W + h * D_HEAD:MIX_W + (h + 1) * D_HEAD]
        v = qkvo_ref[:, 2 * MIX_W + h * D_HEAD:2 * MIX_W + (h + 1) * D_HEAD]
        o = qkvo_ref[:, 3 * MIX_W + h * D_HEAD:3 * MIX_W + (h + 1) * D_HEAD]
        qm, km, vm = q.astype(mxu_dtype), k.astype(mxu_dtype), v.astype(mxu_dtype)
        c_h = c_s[h]
        n_h = n_s[h:h + 1, :]

        s = _dot_nt(qm, km) * w
        num = _dot(s.astype(mxu_dtype), vm) + w_prev * _dot(qm, c_h.astype(mxu_dtype))
        den = (jnp.sum(s, axis=1, keepdims=True)
               + w_prev * jnp.sum(q.astype(F32) * n_h, axis=1, keepdims=True))
        hh = num / jnp.maximum(jnp.abs(den), jnp.exp(-m_t))
        hn = hh * lax.rsqrt(jnp.mean(hh * hh, axis=1, keepdims=True) + NORM_EPS)
        hn = hn * gain[:, h * D_HEAD:(h + 1) * D_HEAD]
        h_ref[:, h * D_HEAD:(h + 1) * D_HEAD] = (hn * _sigmoid(o.astype(F32))).astype(h_ref.dtype)

        m_end = m_t[L - 1:L, :]
        b_end = b_col[L - 1:L, :]
        w_end = jnp.where(src_ok, jnp.exp(b_end - b_col + ig_col - m_end), 0.0)
        decay = jnp.exp(b_end + m_prev - m_end)
        kw = k.astype(F32) * w_end
        c_s[h] = decay * c_h + _dot_tn(kw.astype(mxu_dtype), vm)
        n_s[h:h + 1, :] = decay * n_h + jnp.sum(kw, axis=0, keepdims=True)
        m_s[h:h + 1, :] = jnp.broadcast_to(m_end, (1, LANES))

    @pl.when(c == pl.num_programs(1) - 1)
    def _():
        co_ref[...] = c_s[...]
        no_ref[...] = n_s[...]
        mo_ref[...] = m_s[...]


def mlstm(qkvo, gates, gain, n_seq, n_chunks, chunk, valid, init, out_dtype):
    m = qkvo.shape[0]
    has_init = init is not None
    mxu_dtype = BF16 if chunk >= 16 else F32
    row_map = lambda b, c: (b * n_chunks + c, 0)
    in_specs = [pl.BlockSpec((chunk, 4 * MIX_W), row_map),
                pl.BlockSpec((chunk, LANES), row_map),
                pl.BlockSpec((1, MIX_W), lambda b, c: (0, 0))]
    args = [qkvo, gates, gain.reshape(1, MIX_W).astype(F32)]
    if has_init:
        c0, n0, m0 = init
        in_specs += [pl.BlockSpec((None, N_HEADS, D_HEAD, D_HEAD), lambda b, c: (b, 0, 0, 0)),
                     pl.BlockSpec((None, N_HEADS, D_HEAD), lambda b, c: (b, 0, 0)),
                     pl.BlockSpec((None, N_HEADS, LANES), lambda b, c: (b, 0, 0))]
        args += [c0.astype(F32), n0.astype(F32),
                 jnp.broadcast_to(m0.astype(F32)[:, :, None], (n_seq, N_HEADS, LANES))]
    out_specs = [pl.BlockSpec((chunk, MIX_W), row_map),
                 pl.BlockSpec((None, N_HEADS, D_HEAD, D_HEAD), lambda b, c: (b, 0, 0, 0)),
                 pl.BlockSpec((None, N_HEADS, D_HEAD), lambda b, c: (b, 0, 0)),
                 pl.BlockSpec((None, N_HEADS, LANES), lambda b, c: (b, 0, 0))]
    out_shape = [jax.ShapeDtypeStruct((m, MIX_W), out_dtype),
                 jax.ShapeDtypeStruct((n_seq, N_HEADS, D_HEAD, D_HEAD), F32),
                 jax.ShapeDtypeStruct((n_seq, N_HEADS, D_HEAD), F32),
                 jax.ShapeDtypeStruct((n_seq, N_HEADS, LANES), F32)]
    h, c_out, n_out, m_out = pl.pallas_call(
        functools.partial(_mlstm_kernel, chunk=chunk, valid=valid, has_init=has_init, mxu_dtype=mxu_dtype),
        grid=(n_seq, n_chunks),
        in_specs=in_specs,
        out_specs=out_specs,
        out_shape=out_shape,
        scratch_shapes=[pltpu.VMEM((N_HEADS, D_HEAD, D_HEAD), F32),
                        pltpu.VMEM((N_HEADS, D_HEAD), F32),
                        pltpu.VMEM((N_HEADS, LANES), F32)],
        compiler_params=_params(2),
        name="mlstm_init" if has_init else "mlstm",
    )(*args)
    return h, c_out, n_out, m_out[:, :, 0]


def _top_blocks(gate, n_valid_lane_mask, lane):
    g = jnp.where(n_valid_lane_mask, gate, NEG_INF)
    picks, oks = [], []
    for _ in range(MOBA_TOPK):
        m = jnp.max(g, axis=1, keepdims=True)
        idx = jnp.min(jnp.where(g == m, lane, LANES), axis=1, keepdims=True)
        picks.append(idx)
        oks.append(m > NEG_INF)
        g = jnp.where(lane == idx, NEG_INF, g)
    return picks, oks


def _moba_prompt_kernel(slopes_ref, q_ref, k_ref, v_ref, expand_ref, o_ref, kb_s, vb_s, kbar_s, *, seq):
    h = pl.program_id(1)
    qi = pl.program_id(2)
    n_blocks = seq // MOBA_BLOCK

    @pl.when(qi == 0)
    def _():
        kb_s[...] = k_ref[...].astype(BF16)
        vb_s[...] = v_ref[...].astype(BF16)
        kbar_s[...] = jnp.zeros_like(kbar_s)
        for j in range(n_blocks):
            blk = k_ref[j * MOBA_BLOCK:(j + 1) * MOBA_BLOCK, :]
            kbar_s[j:j + 1, :] = jnp.sum(blk, axis=0, keepdims=True) * (1.0 / MOBA_BLOCK)

    q = q_ref[...]
    gate = _dot_nt(q.astype(F32), kbar_s[...], precision=HIGHEST)
    lane = lax.broadcasted_iota(jnp.int32, (MOBA_BLOCK, LANES), 1)
    picks, oks = _top_blocks(gate, lane < qi, lane)
    sel = jnp.zeros((MOBA_BLOCK, LANES), F32)
    for idx, ok in zip(picks, oks):
        sel = jnp.where((lane == idx) & ok, 1.0, sel)
    sel_keys = _dot(sel.astype(BF16), expand_ref[...])

    q_pos = qi * MOBA_BLOCK + lax.broadcasted_iota(jnp.int32, (MOBA_BLOCK, seq), 0)
    k_pos = lax.broadcasted_iota(jnp.int32, (MOBA_BLOCK, seq), 1)
    own = (k_pos >= qi * MOBA_BLOCK) & (k_pos <= q_pos)
    allowed = (sel_keys > 0.5) | own
    s = _dot_nt(q, kb_s[...])
    s = s - slopes_ref[h] * (q_pos - k_pos).astype(F32)
    s = jnp.where(allowed, s, NEG_INF)
    p = jnp.exp(s - jnp.max(s, axis=-1, keepdims=True))
    o = _dot(p.astype(BF16), vb_s[...])
    o_ref[...] = (o / jnp.sum(p, axis=-1, keepdims=True)).astype(o_ref.dtype)


def moba_prompt(q, k, v, slopes, n_seq, seq):
    n_blocks = seq // MOBA_BLOCK
    expand = (jnp.arange(LANES)[:, None] == (jnp.arange(seq)[None, :] // MOBA_BLOCK)).astype(BF16)
    return pl.pallas_call(
        functools.partial(_moba_prompt_kernel, seq=seq),
        grid=(n_seq, N_HEADS, n_blocks),
        in_specs=[pl.BlockSpec(memory_space=pltpu.SMEM),
                  pl.BlockSpec((MOBA_BLOCK, D_HEAD), lambda b, h, i: (b * n_blocks + i, h)),
                  pl.BlockSpec((seq, D_HEAD), lambda b, h, i: (b, h)),
                  pl.BlockSpec((seq, D_HEAD), lambda b, h, i: (b, h)),
                  pl.BlockSpec((LANES, seq), lambda b, h, i: (0, 0))],
        out_specs=pl.BlockSpec((MOBA_BLOCK, D_HEAD), lambda b, h, i: (b * n_blocks + i, h)),
        out_shape=jax.ShapeDtypeStruct((n_seq * seq, MIX_W), BF16),
        scratch_shapes=[pltpu.VMEM((seq, D_HEAD), BF16), pltpu.VMEM((seq, D_HEAD), BF16),
                        pltpu.VMEM((LANES, D_HEAD), F32)],
        compiler_params=_params(3),
        name="moba_prompt",
    )(slopes, q, k, v, expand)


def _moba_scan_kernel(pt_ref, qbd_ref, k0_ref, k1_ref, kbar_ref, sc_ref, kc_s):
    del pt_ref
    ksum = jnp.zeros((N_HEADS, D_HEAD), F32)
    for half, k_ref in enumerate((k0_ref, k1_ref)):
        ksum = ksum + jnp.sum(k_ref[...], axis=0)
        for h in range(N_HEADS):
            kc_s[half * PAGE_SIZE:(half + 1) * PAGE_SIZE, h * D_HEAD:(h + 1) * D_HEAD] = (
                k_ref[:, h, :].astype(BF16))
    kbar_ref[...] = ksum * (1.0 / MOBA_BLOCK)
    sc_ref[...] = _dot_nt(qbd_ref[...], kc_s[...])


def moba_sample_scan(qbd, cache_k, page_table, n_seq, n_past_blocks):
    rows = qbd.shape[1]
    page_spec = lambda half: pl.BlockSpec(
        (None, PAGE_SIZE, N_HEADS, D_HEAD),
        lambda b, j, pt: (pt[b, PAGES_PER_BLOCK * j + half], 0, 0, 0))
    return pl.pallas_call(
        _moba_scan_kernel,
        grid_spec=pltpu.PrefetchScalarGridSpec(
            num_scalar_prefetch=1,
            grid=(n_seq, n_past_blocks),
            in_specs=[pl.BlockSpec((None, rows, MIX_W), lambda b, j, pt: (b, 0, 0)),
                      page_spec(0), page_spec(1)],
            out_specs=[pl.BlockSpec((None, None, N_HEADS, D_HEAD), lambda b, j, pt: (b, j, 0, 0)),
                       pl.BlockSpec((None, None, rows, MOBA_BLOCK), lambda b, j, pt: (b, j, 0, 0))],
            scratch_shapes=[pltpu.VMEM((MOBA_BLOCK, MIX_W), BF16)],
        ),
        out_shape=[jax.ShapeDtypeStruct((n_seq, n_past_blocks, N_HEADS, D_HEAD), F32),
                   jax.ShapeDtypeStruct((n_seq, n_past_blocks, rows, MOBA_BLOCK), F32)],
        compiler_params=_params(2),
        name="moba_sample_scan",
    )(page_table, qbd, cache_k, cache_k)


def _moba_pick_kernel(qbd_ref, kbar_ref, idx_ref, kb_s, *, n_blocks):
    kb_s[...] = jnp.zeros_like(kb_s)
    for h in range(N_HEADS):
        kb_s[0:n_blocks, h * D_HEAD:(h + 1) * D_HEAD] = kbar_ref[:, h, :]
    gate = _dot_nt(qbd_ref[...].astype(F32), kb_s[...], precision=HIGHEST)
    rows = gate.shape[0]
    lane = lax.broadcasted_iota(jnp.int32, (rows, LANES), 1)
    picks, _ = _top_blocks(gate, lane < n_blocks, lane)
    out = jnp.zeros((rows, LANES), jnp.int32)
    for r, idx in enumerate(picks):
        out = jnp.where(lane == r, idx, out)
    idx_ref[...] = out


def moba_sample_pick(qbd, kbar):
    n_seq, rows, _ = qbd.shape
    n_blocks = kbar.shape[1]
    return pl.pallas_call(
        functools.partial(_moba_pick_kernel, n_blocks=n_blocks),
        grid=(n_seq,),
        in_specs=[pl.BlockSpec((None, rows, MIX_W), lambda b: (b, 0, 0)),
                  pl.BlockSpec((None, n_blocks, N_HEADS, D_HEAD), lambda b: (b, 0, 0, 0))],
        out_specs=pl.BlockSpec((None, rows, LANES), lambda b: (b, 0, 0)),
        out_shape=jax.ShapeDtypeStruct((n_seq, rows, LANES), jnp.int32),
        scratch_shapes=[pltpu.VMEM((LANES, MIX_W), F32)],
        compiler_params=_params(1),
        name="moba_sample_pick",
    )(qbd, kbar)


def _moba_gather_kernel(idx_ref, pt_ref, qbd_ref, sc_ref, kn_ref, vn_ref, slope_ref, cv_hbm, o_ref,
                        vbuf, s_sel, p_sel, kn_s, vn_s, r_s, sem, *, n_tok, n_pages, past_len):
    b = pl.program_id(0)
    rows = N_HEADS * n_tok
    n_sel = rows * MOBA_TOPK

    def v_copy(row, r, half):
        blk = idx_ref[b * n_sel + row * MOBA_TOPK + r]
        page = pt_ref[b * n_pages + PAGES_PER_BLOCK * blk + half]
        return pltpu.make_async_copy(
            cv_hbm.at[page, :, row // n_tok, :],
            vbuf.at[row * MOBA_TOPK + r, pl.ds(half * PAGE_SIZE, PAGE_SIZE), :],
            sem.at[0])

    for row in range(rows):
        for r in range(MOBA_TOPK):
            for half in range(PAGES_PER_BLOCK):
                v_copy(row, r, half).start()

    lane_blk = lax.broadcasted_iota(jnp.int32, (1, MOBA_BLOCK), 1)
    for row in range(rows):
        for r in range(MOBA_TOPK):
            blk = idx_ref[b * n_sel + row * MOBA_TOPK + r]
            s_sel[row:row + 1, r * MOBA_BLOCK:(r + 1) * MOBA_BLOCK] = sc_ref[blk, row:row + 1, :]
            p_sel[row:row + 1, r * MOBA_BLOCK:(r + 1) * MOBA_BLOCK] = (blk * MOBA_BLOCK + lane_blk).astype(F32)

    kn_s[...] = jnp.zeros_like(kn_s)
    vn_s[...] = jnp.zeros_like(vn_s)
    kn_s[0:n_tok, :] = kn_ref[...]
    vn_s[0:n_tok, :] = vn_ref[...]

    slope = slope_ref[...]
    tok = lax.broadcasted_iota(jnp.int32, (rows, 1), 0) % n_tok
    q_pos = (past_len + tok).astype(F32)
    s_past = s_sel[...] - slope * (q_pos - p_sel[...])
    qbd = qbd_ref[...]
    lane = lax.broadcasted_iota(jnp.int32, (rows, LANES), 1)
    s_own = _dot_nt(qbd, kn_s[...].astype(BF16))
    s_own = s_own - slope * (tok - lane).astype(F32)
    s_own = jnp.where(lane <= tok, s_own, NEG_INF)
    m = jnp.maximum(jnp.max(s_past, axis=1, keepdims=True), jnp.max(s_own, axis=1, keepdims=True))
    p_past = jnp.exp(s_past - m)
    p_own = jnp.exp(s_own - m)
    denom = jnp.sum(p_past, axis=1, keepdims=True) + jnp.sum(p_own, axis=1, keepdims=True)
    o_own = _dot(p_own.astype(BF16), vn_s[...].astype(BF16))
    p_past = p_past.astype(BF16)

    for row in range(rows):
        for r in range(MOBA_TOPK):
            for half in range(PAGES_PER_BLOCK):
                v_copy(row, r, half).wait()

    for row in range(rows):
        acc = jnp.zeros((1, D_HEAD), F32)
        for r in range(MOBA_TOPK):
            acc = acc + _dot(p_past[row:row + 1, r * MOBA_BLOCK:(r + 1) * MOBA_BLOCK],
                             vbuf[row * MOBA_TOPK + r].astype(BF16))
        r_s[row:row + 1, :] = acc
    inv = 1.0 / denom
    for h in range(N_HEADS):
        rs = slice(h * n_tok, (h + 1) * n_tok)
        hs = slice(h * D_HEAD, (h + 1) * D_HEAD)
        o_ref[:, hs] = ((r_s[rs, :] + o_own[rs, hs]) * inv[rs, :]).astype(o_ref.dtype)


def moba_sample_gather(idx, page_table, qbd, scores, k_new, v_new, slope_col, cache_v, past_len):
    n_seq, rows, _ = qbd.shape
    n_tok = rows // N_HEADS
    n_blocks = scores.shape[1]
    n_pages = page_table.shape[1]
    kern = functools.partial(_moba_gather_kernel, n_tok=n_tok, n_pages=n_pages, past_len=past_len)
    return pl.pallas_call(
        kern,
        grid_spec=pltpu.PrefetchScalarGridSpec(
            num_scalar_prefetch=2,
            grid=(n_seq,),
            in_specs=[pl.BlockSpec((None, rows, MIX_W), lambda b, i, p: (b, 0, 0)),
                      pl.BlockSpec((None, n_blocks, rows, MOBA_BLOCK), lambda b, i, p: (b, 0, 0, 0)),
                      pl.BlockSpec((None, n_tok, MIX_W), lambda b, i, p: (b, 0, 0)),
                      pl.BlockSpec((None, n_tok, MIX_W), lambda b, i, p: (b, 0, 0)),
                      pl.BlockSpec((rows, 1), lambda b, i, p: (0, 0)),
                      pl.BlockSpec(memory_space=pl.ANY)],
            out_specs=pl.BlockSpec((None, n_tok, MIX_W), lambda b, i, p: (b, 0, 0)),
            scratch_shapes=[pltpu.VMEM((rows * MOBA_TOPK, MOBA_BLOCK, D_HEAD), F32),
                            pltpu.VMEM((rows, MOBA_TOPK * MOBA_BLOCK), F32),
                            pltpu.VMEM((rows, MOBA_TOPK * MOBA_BLOCK), F32),
                            pltpu.VMEM((LANES, MIX_W), F32),
                            pltpu.VMEM((LANES, MIX_W), F32),
                            pltpu.VMEM((rows, D_HEAD), F32),
                            pltpu.SemaphoreType.DMA((1,))],
        ),
        out_shape=jax.ShapeDtypeStruct((n_seq, n_tok, MIX_W), F32),
        compiler_params=_params(1),
        name="moba_sample_gather",
    )(idx, page_table.reshape(-1), qbd, scores, k_new, v_new, slope_col, cache_v)


def _prep_weights(w_in, b_if, w_proj_m, w_proj_a, w_out, w_xq, w_xk, w_xv, w_xo, w_ffn_gate, w_ffn_up, w_ffn_down):
    n_gate = 2 * N_HEADS
    g0 = 4 * MIX_W
    w_main = jnp.concatenate([w_in[:, :g0], w_in[:, g0 + n_gate:]], axis=1).astype(BF16)
    return dict(w_main=w_main, w_if=w_in[:, g0:g0 + n_gate], b_if=b_if,
                w_proj_m=w_proj_m.astype(BF16), w_proj_a=w_proj_a.astype(BF16), w_out=w_out.astype(BF16),
                w_xq=w_xq.astype(BF16), w_xk=w_xk.astype(BF16), w_xv=w_xv.astype(BF16), w_xo=w_xo.astype(BF16),
                w_ffn_gate=w_ffn_gate.astype(BF16), w_ffn_up=w_ffn_up.astype(BF16),
                w_ffn_down=w_ffn_down.astype(BF16))


def _in_proj(x2d, norm_mix, W, tm):
    scale = D_HEAD ** -0.5
    xn = rmsnorm(x2d, norm_mix, tm)
    tn = 1024
    ones = jnp.ones((MIX_W,), F32)
    qkvo_scale = jnp.concatenate([ones, ones * scale, ones, ones])
    odt = BF16 if tm >= 16 and x2d.shape[0] > 128 else F32
    qkvo = matmul(xn, W["w_main"], 0, 4 * MIX_W, tm, tn, odt, qkvo_scale, name="in_proj_mlstm")
    gates = gate_preacts(xn, W["w_if"], W["b_if"], tm)
    q_a = matmul(xn, W["w_main"], 4 * MIX_W, MIX_W, tm, tn, BF16, ones * scale, name="in_proj_moba_q")
    k_a = matmul(xn, W["w_main"], 5 * MIX_W, MIX_W, tm, tn, F32, name="in_proj_moba_k")
    v_a = matmul(xn, W["w_main"], 6 * MIX_W, MIX_W, tm, tn, F32, name="in_proj_moba_v")
    g = matmul(xn, W["w_main"], 7 * MIX_W, 2 * D_MODEL, tm, tn, BF16, name="in_proj_branch_gates")
    return qkvo, gates, q_a, k_a, v_a, g


def _post_mixer(x2d, h_m, h_a, g, mem_k, mem_v, n_seq, W, norm_xattn, norm_ffn, norm_final, tm, tq):
    m = x2d.shape[0]
    u = gated_merge(h_m, h_a, W["w_proj_m"], W["w_proj_a"], g, tm, 1024)
    x1, xn2 = resid_matmul_norm(x2d, u, W["w_out"], norm_xattn, tm, name="out_proj_resid_norm")
    q_dtype = BF16 if tq >= 16 else F32
    xq = matmul(xn2, W["w_xq"], 0, XATTN_W, tm, XATTN_W, q_dtype,
                jnp.full((XATTN_W,), D_HEAD ** -0.5, F32), name="xattn_q_proj")
    xo = cross_attention(xq.reshape(n_seq, m // n_seq, XATTN_W), mem_k, mem_v, tq).reshape(m, XATTN_W)
    x2, xf = resid_matmul_norm(x1, xo.astype(BF16), W["w_xo"], norm_ffn, tm, name="xattn_o_proj_resid_norm")
    return ffn_final_norm(xf, x2, W["w_ffn_gate"], W["w_ffn_up"], W["w_ffn_down"], norm_final, tm, 512)


def kernel(x_prompt, x_sample, cache_k, cache_v, cache_mem_k, cache_mem_v, state_C, state_n, state_m, page_table, mem_prompt, norm_mix, w_in, b_if, mh_gain, w_proj_m, w_proj_a, w_out, norm_xattn, norm_mem, w_xq, w_xk, w_xv, w_xo, norm_ffn, w_ffn_gate, w_ffn_up, w_ffn_down, norm_final):
    n_layers = w_in.shape[0]
    assert n_layers == 1
    B, T, _ = x_prompt.shape
    DB, TS, _ = x_sample.shape
    n_pages = page_table.shape[1]
    past_len = n_pages * PAGE_SIZE
    assert past_len % MOBA_BLOCK == 0 and TS <= MOBA_BLOCK and T % MOBA_BLOCK == 0
    W = _prep_weights(w_in[0], b_if[0], w_proj_m[0], w_proj_a[0], w_out[0], w_xq[0], w_xk[0], w_xv[0], w_xo[0],
                      w_ffn_gate[0], w_ffn_up[0], w_ffn_down[0])
    slopes = jnp.exp2(-8.0 * jnp.arange(1, N_HEADS + 1, dtype=F32) / N_HEADS)

    TM_P = 512
    xp = x_prompt.reshape(B * T, D_MODEL)
    qkvo, gates, q_a, k_a, v_a, g = _in_proj(xp, norm_mix[0], W, TM_P)
    chunk = 256
    h_m, c_p, n_p, m_p = mlstm(qkvo, gates, mh_gain[0], B, T // chunk, chunk, chunk, None, BF16)
    h_a = moba_prompt(q_a, k_a, v_a, slopes, B, T)
    mn = rmsnorm(mem_prompt.reshape(B * N_MEM, D_MODEL), norm_mem[0], 512)
    mk_p = matmul(mn, W["w_xk"], 0, XATTN_W, 512, XATTN_W, F32, name="mem_k_proj")
    mv_p = matmul(mn, W["w_xv"], 0, XATTN_W, 512, XATTN_W, F32, name="mem_v_proj")
    y_p = _post_mixer(xp, h_m, h_a, g, mk_p.reshape(B, N_MEM, XATTN_W), mv_p.reshape(B, N_MEM, XATTN_W), B, W,
                      norm_xattn[0], norm_ffn[0], norm_final, TM_P, 512)

    MS = DB * TS
    xs = x_sample.reshape(MS, D_MODEL)
    qkvo_s, gates_s, q_as, k_as, v_as, g_s = _in_proj(xs, norm_mix[0], W, MS)
    pad_t = 8
    pad_rows = lambda a: jnp.pad(a.reshape(DB, TS, -1), ((0, 0), (0, pad_t - TS), (0, 0))).reshape(DB * pad_t, -1)
    h_ms, c_s, n_s, m_s = mlstm(pad_rows(qkvo_s), pad_rows(gates_s), mh_gain[0], DB, 1, pad_t, TS,
                                (state_C[0], state_n[0], state_m[0]), F32)
    h_ms = h_ms.reshape(DB, pad_t, MIX_W)[:, :TS].reshape(MS, MIX_W).astype(BF16)

    q4 = q_as.reshape(DB, TS, N_HEADS, D_HEAD)
    eye = jnp.eye(N_HEADS, dtype=BF16)
    qbd = (q4.transpose(0, 2, 1, 3)[:, :, :, None, :] * eye[None, :, None, :, None]).reshape(DB, N_HEADS * TS, MIX_W)
    ck = cache_k.reshape(cache_k.shape[1:])
    cv = cache_v.reshape(cache_v.shape[1:])
    n_past_blocks = past_len // MOBA_BLOCK
    kbar, scores = moba_sample_scan(qbd, ck, page_table, DB, n_past_blocks)
    idx = moba_sample_pick(qbd, kbar)[:, :, :MOBA_TOPK].reshape(-1)
    slope_col = jnp.repeat(slopes, TS).reshape(N_HEADS * TS, 1)
    h_as = moba_sample_gather(idx, page_table, qbd, scores, k_as.reshape(DB, TS, MIX_W), v_as.reshape(DB, TS, MIX_W),
                              slope_col, cv, past_len)
    h_as = h_as.reshape(MS, MIX_W).astype(BF16)
    y_s = _post_mixer(xs, h_ms, h_as, g_s, cache_mem_k[0].reshape(DB, N_MEM, XATTN_W),
                      cache_mem_v[0].reshape(DB, N_MEM, XATTN_W), DB, W, norm_xattn[0], norm_ffn[0], norm_final, MS, TS)

    kv_shape_p = (1, B, T, N_HEADS, D_HEAD)
    kv_shape_s = (1, DB, TS, N_HEADS, D_HEAD)
    mem_shape = (1, B, N_MEM, XATTN_HEADS, D_HEAD)
    return (y_p.reshape(B, T, D_MODEL), y_s.reshape(DB, TS, D_MODEL),
            k_a.reshape(kv_shape_p), v_a.reshape(kv_shape_p),
            c_p[None].astype(state_C.dtype), n_p[None].astype(state_n.dtype), m_p[None].astype(state_m.dtype),
            mk_p.reshape(mem_shape), mv_p.reshape(mem_shape),
            k_as.reshape(kv_shape_s).astype(cache_k.dtype), v_as.reshape(kv_shape_s).astype(cache_v.dtype),
            c_s[None].astype(state_C.dtype), n_s[None].astype(state_n.dtype), m_s[None].astype(state_m.dtype))
```

```python
import functools

import jax
import jax.numpy as jnp
from jax import lax
from jax.experimental import pallas as pl
from jax.experimental.pallas import tpu as pltpu

F32 = jnp.float32
BF16 = jnp.bfloat16

D_MODEL = 2048
N_HEADS = 8
D_HEAD = 128
MIX_W = N_HEADS * D_HEAD
MOBA_BLOCK = 256
MOBA_TOPK = 3
PAGE_SIZE = 128
PAGES_PER_BLOCK = MOBA_BLOCK // PAGE_SIZE
XATTN_HEADS = 4
XATTN_W = XATTN_HEADS * D_HEAD
N_MEM = 256
NORM_EPS = 1e-6
LANES = 128
NEG_INF = float("-inf")
VMEM_LIMIT = 52 * 1024 * 1024

HIGHEST = lax.Precision.HIGHEST


def _params(n_axes, vmem=VMEM_LIMIT):
    return pltpu.CompilerParams(dimension_semantics=("arbitrary",) * n_axes, vmem_limit_bytes=vmem)


def _dot(a, b, precision=None):
    return jnp.dot(a, b, preferred_element_type=F32, precision=precision)


def _dot_nt(a, b, precision=None):
    return lax.dot_general(a, b, (((1,), (1,)), ((), ())), preferred_element_type=F32, precision=precision)


def _dot_tn(a, b, precision=None):
    return lax.dot_general(a, b, (((0,), (0,)), ((), ())), preferred_element_type=F32, precision=precision)


def _rms(x, g):
    return x * lax.rsqrt(jnp.mean(x * x, axis=-1, keepdims=True) + NORM_EPS) * g


def _sigmoid(x):
    return 1.0 / (1.0 + jnp.exp(-x))


def _rmsnorm_kernel(x_ref, g_ref, o_ref):
    o_ref[...] = _rms(x_ref[...].astype(F32), g_ref[...]).astype(o_ref.dtype)


def rmsnorm(x, g, tm, out_dtype=BF16):
    m, d = x.shape
    return pl.pallas_call(
        _rmsnorm_kernel,
        grid=(m // tm,),
        in_specs=[pl.BlockSpec((tm, d), lambda i: (i, 0)), pl.BlockSpec((1, d), lambda i: (0, 0))],
        out_specs=pl.BlockSpec((tm, d), lambda i: (i, 0)),
        out_shape=jax.ShapeDtypeStruct((m, d), out_dtype),
        compiler_params=_params(1),
        name="rmsnorm",
    )(x, g.reshape(1, d).astype(F32))


def _mm_kernel(a_ref, w_ref, s_ref, o_ref):
    acc = _dot(a_ref[...], w_ref[...])
    o_ref[...] = (acc * s_ref[...]).astype(o_ref.dtype)


def matmul(a, w, col_off, n_cols, tm, tn, out_dtype, col_scale=None, name="matmul"):
    m, k = a.shape
    assert col_off % tn == 0 and n_cols % tn == 0 and m % tm == 0
    off = col_off // tn
    if col_scale is None:
        col_scale = jnp.ones((n_cols,), F32)
    return pl.pallas_call(
        _mm_kernel,
        grid=(m // tm, n_cols // tn),
        in_specs=[pl.BlockSpec((tm, k), lambda i, j: (i, 0)),
                  pl.BlockSpec((k, tn), lambda i, j: (0, j + off)),
                  pl.BlockSpec((1, tn), lambda i, j: (0, j))],
        out_specs=pl.BlockSpec((tm, tn), lambda i, j: (i, j)),
        out_shape=jax.ShapeDtypeStruct((m, n_cols), out_dtype),
        compiler_params=_params(2),
        name=name,
    )(a, w, col_scale.reshape(1, n_cols).astype(F32))


def _gates_kernel(a_ref, w_ref, b_ref, o_ref):
    a = a_ref[...]
    w = w_ref[...]
    w_hi = w.astype(BF16)
    rest = w - w_hi.astype(F32)
    w_mid = rest.astype(BF16)
    w_lo = (rest - w_mid.astype(F32)).astype(BF16)
    o_ref[...] = (_dot(a, w_lo) + _dot(a, w_mid)) + _dot(a, w_hi) + b_ref[...]


def gate_preacts(xn, w_if, b_if, tm):
    m, k = xn.shape
    n = w_if.shape[1]
    w_pad = jnp.pad(w_if.astype(F32), ((0, 0), (0, LANES - n)))
    b_pad = jnp.pad(b_if.astype(F32), (0, LANES - n)).reshape(1, LANES)
    return pl.pallas_call(
        _gates_kernel,
        grid=(m // tm,),
        in_specs=[pl.BlockSpec((tm, k), lambda i: (i, 0)),
                  pl.BlockSpec((k, LANES), lambda i: (0, 0)),
                  pl.BlockSpec((1, LANES), lambda i: (0, 0))],
        out_specs=pl.BlockSpec((tm, LANES), lambda i: (i, 0)),
        out_shape=jax.ShapeDtypeStruct((m, LANES), F32),
        compiler_params=_params(1),
        name="gate_preacts",
    )(xn, w_pad, b_pad)


def _proj_kernel(hm_ref, ha_ref, wm_ref, wa_ref, gm_ref, ga_ref, o_ref):
    pm = _dot(hm_ref[...], wm_ref[...])
    pa = _dot(ha_ref[...], wa_ref[...])
    u = _sigmoid(gm_ref[...].astype(F32)) * pm + _sigmoid(ga_ref[...].astype(F32)) * pa
    o_ref[...] = u.astype(o_ref.dtype)


def gated_merge(hm, ha, wm, wa, g, tm, tn):
    m, k = hm.shape
    n = wm.shape[1]
    nb = n // tn
    return pl.pallas_call(
        _proj_kernel,
        grid=(m // tm, nb),
        in_specs=[pl.BlockSpec((tm, k), lambda i, j: (i, 0)),
                  pl.BlockSpec((tm, k), lambda i, j: (i, 0)),
                  pl.BlockSpec((k, tn), lambda i, j: (0, j)),
                  pl.BlockSpec((k, tn), lambda i, j: (0, j)),
                  pl.BlockSpec((tm, tn), lambda i, j: (i, j)),
                  pl.BlockSpec((tm, tn), lambda i, j: (i, j + nb))],
        out_specs=pl.BlockSpec((tm, tn), lambda i, j: (i, j)),
        out_shape=jax.ShapeDtypeStruct((m, n), BF16),
        compiler_params=_params(2),
        name="gated_merge",
    )(hm, ha, wm, wa, g, g)


def _resid_mm_norm_kernel(x_ref, a_ref, w_ref, g_ref, xo_ref, xn_ref):
    x1 = x_ref[...] + _dot(a_ref[...], w_ref[...])
    xo_ref[...] = x1
    xn_ref[...] = _rms(x1, g_ref[...]).astype(xn_ref.dtype)


def resid_matmul_norm(x, a, w, g, tm, name):
    m, d = x.shape
    k = a.shape[1]
    return pl.pallas_call(
        _resid_mm_norm_kernel,
        grid=(m // tm,),
        in_specs=[pl.BlockSpec((tm, d), lambda i: (i, 0)),
                  pl.BlockSpec((tm, k), lambda i: (i, 0)),
                  pl.BlockSpec((k, d), lambda i: (0, 0)),
                  pl.BlockSpec((1, d), lambda i: (0, 0))],
        out_specs=[pl.BlockSpec((tm, d), lambda i: (i, 0)), pl.BlockSpec((tm, d), lambda i: (i, 0))],
        out_shape=[jax.ShapeDtypeStruct((m, d), F32), jax.ShapeDtypeStruct((m, d), BF16)],
        compiler_params=_params(1),
        name=name,
    )(x, a, w, g.reshape(1, d).astype(F32))


def _ffn_kernel(xf_ref, x_ref, wg_ref, wu_ref, wd_ref, gf_ref, y_ref, acc_ref):
    k = pl.program_id(1)

    @pl.when(k == 0)
    def _():
        acc_ref[...] = jnp.zeros_like(acc_ref)

    xf = xf_ref[...]
    g = _dot(xf, wg_ref[...])
    u = _dot(xf, wu_ref[...])
    hidden = (g * _sigmoid(g)) * u
    acc_ref[...] += _dot(hidden.astype(BF16), wd_ref[...])

    @pl.when(k == pl.num_programs(1) - 1)
    def _():
        y_ref[...] = _rms(x_ref[...] + acc_ref[...], gf_ref[...])


def ffn_final_norm(xf, x, wg, wu, wd, g_final, tm, tf):
    m, d = x.shape
    f = wg.shape[1]
    return pl.pallas_call(
        _ffn_kernel,
        grid=(m // tm, f // tf),
        in_specs=[pl.BlockSpec((tm, d), lambda i, k: (i, 0)),
                  pl.BlockSpec((tm, d), lambda i, k: (i, 0)),
                  pl.BlockSpec((d, tf), lambda i, k: (0, k)),
                  pl.BlockSpec((d, tf), lambda i, k: (0, k)),
                  pl.BlockSpec((tf, d), lambda i, k: (k, 0)),
                  pl.BlockSpec((1, d), lambda i, k: (0, 0))],
        out_specs=pl.BlockSpec((tm, d), lambda i, k: (i, 0)),
        out_shape=jax.ShapeDtypeStruct((m, d), F32),
        scratch_shapes=[pltpu.VMEM((tm, d), F32)],
        compiler_params=_params(2),
        name="ffn_final_norm",
    )(xf, x, wg, wu, wd, g_final.reshape(1, d).astype(F32))


def _xattn_kernel(q_ref, k_ref, v_ref, o_ref, *, mxu_dtype):
    q = q_ref[...]
    for h in range(XATTN_HEADS):
        sl = slice(h * D_HEAD, (h + 1) * D_HEAD)
        s = _dot_nt(q[:, sl].astype(mxu_dtype), k_ref[:, sl].astype(mxu_dtype))
        p = jnp.exp(s - jnp.max(s, axis=-1, keepdims=True))
        o = _dot(p.astype(mxu_dtype), v_ref[:, sl].astype(mxu_dtype))
        o_ref[:, sl] = (o / jnp.sum(p, axis=-1, keepdims=True)).astype(o_ref.dtype)


def cross_attention(q, mem_k, mem_v, tq):
    b, t, w = q.shape
    mxu_dtype = BF16 if tq >= 16 else F32
    return pl.pallas_call(
        functools.partial(_xattn_kernel, mxu_dtype=mxu_dtype),
        grid=(b, t // tq),
        in_specs=[pl.BlockSpec((None, tq, w), lambda i, j: (i, j, 0)),
                  pl.BlockSpec((None, N_MEM, w), lambda i, j: (i, 0, 0)),
                  pl.BlockSpec((None, N_MEM, w), lambda i, j: (i, 0, 0))],
        out_specs=pl.BlockSpec((None, tq, w), lambda i, j: (i, j, 0)),
        out_shape=jax.ShapeDtypeStruct((b, t, w), q.dtype),
        compiler_params=_params(2),
        name="cross_attention",
    )(q, mem_k, mem_v)


def _mlstm_kernel(*refs, chunk, valid, has_init, mxu_dtype):
    if has_init:
        qkvo_ref, g_ref, gain_ref, c0_ref, n0_ref, m0_ref, h_ref, co_ref, no_ref, mo_ref, c_s, n_s, m_s = refs
    else:
        qkvo_ref, g_ref, gain_ref, h_ref, co_ref, no_ref, mo_ref, c_s, n_s, m_s = refs
    c = pl.program_id(1)
    L = chunk

    @pl.when(c == 0)
    def _():
        if has_init:
            c_s[...] = c0_ref[...]
            n_s[...] = n0_ref[...]
            m_s[...] = m0_ref[...]
        else:
            c_s[...] = jnp.zeros_like(c_s)
            n_s[...] = jnp.zeros_like(n_s)
            m_s[...] = jnp.zeros_like(m_s)

    g = g_ref[...]
    row = lax.broadcasted_iota(jnp.int32, (L, LANES), 0)
    lane = lax.broadcasted_iota(jnp.int32, (L, LANES), 1)
    log_f = jnp.minimum(g, 0.0) - jnp.log(1.0 + jnp.exp(-jnp.abs(g)))
    is_f = (lane >= N_HEADS) & (lane < 2 * N_HEADS) & (row < valid)
    log_f = jnp.where(is_f, log_f, 0.0)
    rr = lax.broadcasted_iota(jnp.int32, (L, L), 0)
    cc = lax.broadcasted_iota(jnp.int32, (L, L), 1)
    tril = (cc <= rr).astype(F32)
    b_all = _dot(tril, log_f, precision=HIGHEST)
    ig_shift = pltpu.roll(g, N_HEADS, axis=1)
    d_all = jnp.where((lane >= N_HEADS) & (lane < 2 * N_HEADS), b_all - ig_shift, 0.0)
    keep = (cc <= rr) & (cc < valid)
    src_ok = lax.broadcasted_iota(jnp.int32, (L, 1), 0) < valid
    gain = gain_ref[...]

    for h in range(N_HEADS):
        col = N_HEADS + h
        sel = (lane == col).astype(F32)
        d_row = _dot_nt(sel, d_all, precision=HIGHEST)
        b_col = b_all[:, col:col + 1]
        ig_col = g[:, h:h + 1]
        m_prev = m_s[h:h + 1, 0:1]
        logw = jnp.where(keep, b_col - d_row, NEG_INF)
        log_prev = b_col + m_prev
        m_t = jnp.maximum(log_prev, jnp.max(logw, axis=1, keepdims=True))
        w = jnp.exp(logw - m_t)
        w_prev = jnp.exp(log_prev - m_t)

        q = qkvo_ref[:, h * D_HEAD:(h + 1) * D_HEAD]
        k = qkvo_ref[:, MIX_W + h * D_HEAD:MIX_W + (h + 1) * D_HEAD]
        v = qkvo_ref[:, 2 * MIX_W + h * D_HEAD:2 * MIX_W + (h + 1) * D_HEAD]
        o = qkvo_ref[:, 3 * MIX_W + h * D_HEAD:3 * MIX_W + (h + 1) * D_HEAD]
        qm, km, vm = q.astype(mxu_dtype), k.astype(mxu_dtype), v.astype(mxu_dtype)
        c_h = c_s[h]
        n_h = n_s[h:h + 1, :]

        s = _dot_nt(qm, km) * w
        num = _dot(s.astype(mxu_dtype), vm) + w_prev * _dot(qm, c_h.astype(mxu_dtype))
        den = (jnp.sum(s, axis=1, keepdims=True)
               + w_prev * jnp.sum(q.astype(F32) * n_h, axis=1, keepdims=True))
        hh = num / jnp.maximum(jnp.abs(den), jnp.exp(-m_t))
        hn = hh * lax.rsqrt(jnp.mean(hh * hh, axis=1, keepdims=True) + NORM_EPS)
        hn = hn * gain[:, h * D_HEAD:(h + 1) * D_HEAD]
        h_ref[:, h * D_HEAD:(h + 1) * D_HEAD] = (hn * _sigmoid(o.astype(F32))).astype(h_ref.dtype)

        m_end = m_t[L - 1:L, :]
        b_end = b_col[L - 1:L, :]
        w_end = jnp.where(src_ok, jnp.exp(b_end - b_col + ig_col - m_end), 0.0)
        decay = jnp.exp(b_end + m_prev - m_end)
        kw = k.astype(F32) * w_end
        c_s[h] = decay * c_h + _dot_tn(kw.astype(mxu_dtype), vm)
        n_s[h:h + 1, :] = decay * n_h + jnp.sum(kw, axis=0, keepdims=True)
        m_s[h:h + 1, :] = jnp.broadcast_to(m_end, (1, LANES))

    @pl.when(c == pl.num_programs(1) - 1)
    def _():
        co_ref[...] = c_s[...]
        no_ref[...] = n_s[...]
        mo_ref[...] = m_s[...]


def mlstm(qkvo, gates, gain, n_seq, n_chunks, chunk, valid, init, out_dtype):
    m = qkvo.shape[0]
    has_init = init is not None
    mxu_dtype = BF16 if chunk >= 16 else F32
    row_map = lambda b, c: (b * n_chunks + c, 0)
    in_specs = [pl.BlockSpec((chunk, 4 * MIX_W), row_map),
                pl.BlockSpec((chunk, LANES), row_map),
                pl.BlockSpec((1, MIX_W), lambda b, c: (0, 0))]
    args = [qkvo, gates, gain.reshape(1, MIX_W).astype(F32)]
    if has_init:
        c0, n0, m0 = init
        in_specs += [pl.BlockSpec((None, N_HEADS, D_HEAD, D_HEAD), lambda b, c: (b, 0, 0, 0)),
                     pl.BlockSpec((None, N_HEADS, D_HEAD), lambda b, c: (b, 0, 0)),
                     pl.BlockSpec((None, N_HEADS, LANES), lambda b, c: (b, 0, 0))]
        args += [c0.astype(F32), n0.astype(F32),
                 jnp.broadcast_to(m0.astype(F32)[:, :, None], (n_seq, N_HEADS, LANES))]
    out_specs = [pl.BlockSpec((chunk, MIX_W), row_map),
                 pl.BlockSpec((None, N_HEADS, D_HEAD, D_HEAD), lambda b, c: (b, 0, 0, 0)),
                 pl.BlockSpec((None, N_HEADS, D_HEAD), lambda b, c: (b, 0, 0)),
                 pl.BlockSpec((None, N_HEADS, LANES), lambda b, c: (b, 0, 0))]
    out_shape = [jax.ShapeDtypeStruct((m, MIX_W), out_dtype),
                 jax.ShapeDtypeStruct((n_seq, N_HEADS, D_HEAD, D_HEAD), F32),
                 jax.ShapeDtypeStruct((n_seq, N_HEADS, D_HEAD), F32),
                 jax.ShapeDtypeStruct((n_seq, N_HEADS, LANES), F32)]
    h, c_out, n_out, m_out = pl.pallas_call(
        functools.partial(_mlstm_kernel, chunk=chunk, valid=valid, has_init=has_init, mxu_dtype=mxu_dtype),
        grid=(n_seq, n_chunks),
        in_specs=in_specs,
        out_specs=out_specs,
        out_shape=out_shape,
        scratch_shapes=[pltpu.VMEM((N_HEADS, D_HEAD, D_HEAD), F32),
                        pltpu.VMEM((N_HEADS, D_HEAD), F32),
                        pltpu.VMEM((N_HEADS, LANES), F32)],
        compiler_params=_params(2),
        name="mlstm_init" if has_init else "mlstm",
    )(*args)
    return h, c_out, n_out, m_out[:, :, 0]


def _top_blocks(gate, n_valid_lane_mask, lane):
    g = jnp.where(n_valid_lane_mask, gate, NEG_INF)
    lane_f = lane.astype(F32)
    picks, oks = [], []
    for _ in range(MOBA_TOPK):
        m = jnp.max(g, axis=1, keepdims=True)
        idx = jnp.min(jnp.where(g == m, lane_f, float(LANES)), axis=1, keepdims=True)
        picks.append(idx.astype(jnp.int32))
        oks.append(m > NEG_INF)
        g = jnp.where(lane_f == idx, NEG_INF, g)
    return picks, oks


def _moba_prompt_kernel(slopes_ref, q_ref, k_ref, v_ref, o_ref, kb_s, vb_s, kbar_s, *, seq):
    slope = slopes_ref[pl.program_id(1)]
    n_blocks = seq // MOBA_BLOCK
    kb_s[...] = k_ref[...].astype(BF16)
    vb_s[...] = v_ref[...].astype(BF16)
    kbar_s[...] = jnp.zeros_like(kbar_s)
    for j in range(n_blocks):
        blk = k_ref[j * MOBA_BLOCK:(j + 1) * MOBA_BLOCK, :]
        kbar_s[j:j + 1, :] = jnp.sum(blk, axis=0, keepdims=True) * (1.0 / MOBA_BLOCK)

    rr = lax.broadcasted_iota(jnp.int32, (MOBA_BLOCK, MOBA_BLOCK), 0)
    cc = lax.broadcasted_iota(jnp.int32, (MOBA_BLOCK, MOBA_BLOCK), 1)
    tile_bias = slope * (rr - cc).astype(F32)
    causal = cc <= rr
    lane = lax.broadcasted_iota(jnp.int32, (MOBA_BLOCK, LANES), 1)

    for qi in range(n_blocks):
        rows = slice(qi * MOBA_BLOCK, (qi + 1) * MOBA_BLOCK)
        q = q_ref[rows, :]
        s_own = jnp.where(causal, _dot_nt(q, kb_s[rows, :]) - tile_bias, NEG_INF)
        m = jnp.max(s_own, axis=1, keepdims=True)
        past = []
        if qi > 0:
            gate = _dot_nt(q.astype(F32), kbar_s[...], precision=HIGHEST)
            picks, oks = _top_blocks(gate, lane < qi, lane)
            chosen = jnp.zeros((MOBA_BLOCK, LANES), jnp.bool_)
            for idx, ok in zip(picks, oks):
                chosen = chosen | ((lane == idx) & ok)
            block_bias = jnp.where(chosen, (-MOBA_BLOCK * slope) * (qi - lane).astype(F32), NEG_INF)
            s_past = _dot_nt(q, kb_s[0:qi * MOBA_BLOCK, :])
            for j in range(qi):
                sj = (s_past[:, j * MOBA_BLOCK:(j + 1) * MOBA_BLOCK] - tile_bias) + block_bias[:, j:j + 1]
                past.append(sj)
                m = jnp.maximum(m, jnp.max(sj, axis=1, keepdims=True))
        p = jnp.concatenate([jnp.exp(sj - m) for sj in past] + [jnp.exp(s_own - m)], axis=1)
        o = _dot(p.astype(BF16), vb_s[0:(qi + 1) * MOBA_BLOCK, :])
        o_ref[rows, :] = (o / jnp.sum(p, axis=1, keepdims=True)).astype(o_ref.dtype)


def moba_prompt(q, k, v, slopes, n_seq, seq):
    head_spec = pl.BlockSpec((seq, D_HEAD), lambda b, h: (b, h))
    return pl.pallas_call(
        functools.partial(_moba_prompt_kernel, seq=seq),
        grid=(n_seq, N_HEADS),
        in_specs=[pl.BlockSpec(memory_space=pltpu.SMEM), head_spec, head_spec, head_spec],
        out_specs=head_spec,
        out_shape=jax.ShapeDtypeStruct((n_seq * seq, MIX_W), BF16),
        scratch_shapes=[pltpu.VMEM((seq, D_HEAD), BF16), pltpu.VMEM((seq, D_HEAD), BF16),
                        pltpu.VMEM((LANES, D_HEAD), F32)],
        compiler_params=_params(2),
        name="moba_prompt",
    )(slopes, q, k, v)


SCAN_BLOCKS_PER_STEP = 4


def _moba_scan_kernel(pt_ref, qbd_ref, *refs):
    del pt_ref
    n_pages = SCAN_BLOCKS_PER_STEP * PAGES_PER_BLOCK
    page_refs, (kbar_ref, sc_ref, kc_s) = refs[:n_pages], refs[n_pages:]
    qbd = qbd_ref[...]
    for blk in range(SCAN_BLOCKS_PER_STEP):
        ksum = jnp.zeros((N_HEADS, D_HEAD), F32)
        for half in range(PAGES_PER_BLOCK):
            k_ref = page_refs[blk * PAGES_PER_BLOCK + half]
            ksum = ksum + jnp.sum(k_ref[...].reshape(PAGE_SIZE, N_HEADS, D_HEAD), axis=0)
            for h in range(N_HEADS):
                kc_s[blk, half * PAGE_SIZE:(half + 1) * PAGE_SIZE, h * D_HEAD:(h + 1) * D_HEAD] = (
                    k_ref[pl.ds(h, PAGE_SIZE, stride=N_HEADS), :].astype(BF16))
        kbar_ref[blk] = ksum * (1.0 / MOBA_BLOCK)
        sc_ref[blk] = _dot_nt(qbd, kc_s[blk])


def moba_sample_scan(qbd, cache_k, page_table, n_seq, n_past_blocks):
    rows = qbd.shape[1]
    n_phys = cache_k.shape[0]
    nb = SCAN_BLOCKS_PER_STEP
    n_pages = nb * PAGES_PER_BLOCK
    assert n_past_blocks % nb == 0
    cache_k = cache_k.reshape(n_phys, PAGE_SIZE * N_HEADS, D_HEAD)
    page_spec = lambda p: pl.BlockSpec(
        (None, PAGE_SIZE * N_HEADS, D_HEAD),
        lambda b, j, pt: (pt[b, n_pages * j + p], 0, 0))
    return pl.pallas_call(
        _moba_scan_kernel,
        grid_spec=pltpu.PrefetchScalarGridSpec(
            num_scalar_prefetch=1,
            grid=(n_seq, n_past_blocks // nb),
            in_specs=[pl.BlockSpec((None, rows, MIX_W), lambda b, j, pt: (b, 0, 0))]
                     + [page_spec(p) for p in range(n_pages)],
            out_specs=[pl.BlockSpec((None, nb, N_HEADS, D_HEAD), lambda b, j, pt: (b, j, 0, 0)),
                       pl.BlockSpec((None, nb, rows, MOBA_BLOCK), lambda b, j, pt: (b, j, 0, 0))],
            scratch_shapes=[pltpu.VMEM((nb, MOBA_BLOCK, MIX_W), BF16)],
        ),
        out_shape=[jax.ShapeDtypeStruct((n_seq, n_past_blocks, N_HEADS, D_HEAD), F32),
                   jax.ShapeDtypeStruct((n_seq, n_past_blocks, rows, MOBA_BLOCK), F32)],
        compiler_params=_params(2),
        name="moba_sample_scan",
    )(page_table, qbd, *([cache_k] * n_pages))


def _moba_pick_kernel(qbd_ref, kbar_ref, idx_ref, kb_s, *, n_blocks):
    kb_s[...] = jnp.zeros_like(kb_s)
    for h in range(N_HEADS):
        kb_s[0:n_blocks, h * D_HEAD:(h + 1) * D_HEAD] = kbar_ref[:, h, :]
    gate = _dot_nt(qbd_ref[...].astype(F32), kb_s[...], precision=HIGHEST)
    rows = gate.shape[0]
    lane = lax.broadcasted_iota(jnp.int32, (rows, LANES), 1)
    picks, _ = _top_blocks(gate, lane < n_blocks, lane)
    out = jnp.zeros((rows, LANES), jnp.int32)
    for r, idx in enumerate(picks):
        out = jnp.where(lane == r, idx, out)
    idx_ref[...] = out


def moba_sample_pick(qbd, kbar):
    n_seq, rows, _ = qbd.shape
    n_blocks = kbar.shape[1]
    return pl.pallas_call(
        functools.partial(_moba_pick_kernel, n_blocks=n_blocks),
        grid=(n_seq,),
        in_specs=[pl.BlockSpec((None, rows, MIX_W), lambda b: (b, 0, 0)),
                  pl.BlockSpec((None, n_blocks, N_HEADS, D_HEAD), lambda b: (b, 0, 0, 0))],
        out_specs=pl.BlockSpec((None, rows, LANES), lambda b: (b, 0, 0)),
        out_shape=jax.ShapeDtypeStruct((n_seq, rows, LANES), jnp.int32),
        scratch_shapes=[pltpu.VMEM((LANES, MIX_W), F32)],
        compiler_params=_params(1),
        name="moba_sample_pick",
    )(qbd, kbar)


def _moba_gather_kernel(idx_ref, pt_ref, qbd_ref, sc_ref, kn_ref, vn_ref, slope_ref, cv_hbm, o_ref,
                        vbuf, s_sel, p_sel, kn_s, vn_s, r_s, sem, *, n_tok, n_pages, past_len):
    b = pl.program_id(0)
    rows = N_HEADS * n_tok
    n_sel = rows * MOBA_TOPK

    slot = b % 2

    def v_copy(seq, buf, row, r, half):
        blk = idx_ref[seq * n_sel + row * MOBA_TOPK + r]
        page = pt_ref[seq * n_pages + PAGES_PER_BLOCK * blk + half]
        return pltpu.make_async_copy(
            cv_hbm.at[page, :, row // n_tok, :],
            vbuf.at[buf, row * MOBA_TOPK + r, pl.ds(half * PAGE_SIZE, PAGE_SIZE), :],
            sem.at[buf])

    def start_gather(seq, buf):
        for row in range(rows):
            for r in range(MOBA_TOPK):
                for half in range(PAGES_PER_BLOCK):
                    v_copy(seq, buf, row, r, half).start()

    @pl.when(b == 0)
    def _():
        start_gather(0, 0)

    @pl.when(b + 1 < pl.num_programs(0))
    def _():
        start_gather(b + 1, 1 - slot)

    lane_blk = lax.broadcasted_iota(jnp.int32, (1, MOBA_BLOCK), 1)
    for row in range(rows):
        for r in range(MOBA_TOPK):
            blk = idx_ref[b * n_sel + row * MOBA_TOPK + r]
            s_sel[row:row + 1, r * MOBA_BLOCK:(r + 1) * MOBA_BLOCK] = sc_ref[blk, row:row + 1, :]
            p_sel[row:row + 1, r * MOBA_BLOCK:(r + 1) * MOBA_BLOCK] = (blk * MOBA_BLOCK + lane_blk).astype(F32)

    kn_s[...] = jnp.zeros_like(kn_s)
    vn_s[...] = jnp.zeros_like(vn_s)
    kn_s[0:n_tok, :] = kn_ref[...]
    vn_s[0:n_tok, :] = vn_ref[...]

    slope = slope_ref[...]
    tok = lax.broadcasted_iota(jnp.int32, (rows, 1), 0) % n_tok
    q_pos = (past_len + tok).astype(F32)
    s_past = s_sel[...] - slope * (q_pos - p_sel[...])
    qbd = qbd_ref[...]
    lane = lax.broadcasted_iota(jnp.int32, (rows, LANES), 1)
    s_own = _dot_nt(qbd, kn_s[...].astype(BF16))
    s_own = s_own - slope * (tok - lane).astype(F32)
    s_own = jnp.where(lane <= tok, s_own, NEG_INF)
    m = jnp.maximum(jnp.max(s_past, axis=1, keepdims=True), jnp.max(s_own, axis=1, keepdims=True))
    p_past = jnp.exp(s_past - m)
    p_own = jnp.exp(s_own - m)
    denom = jnp.sum(p_past, axis=1, keepdims=True) + jnp.sum(p_own, axis=1, keepdims=True)
    o_own = _dot(p_own.astype(BF16), vn_s[...].astype(BF16))
    p_past = p_past.astype(BF16)

    for row in range(rows):
        for r in range(MOBA_TOPK):
            for half in range(PAGES_PER_BLOCK):
                v_copy(b, slot, row, r, half).wait()

    for row in range(rows):
        acc = jnp.zeros((1, D_HEAD), F32)
        for r in range(MOBA_TOPK):
            acc = acc + _dot(p_past[row:row + 1, r * MOBA_BLOCK:(r + 1) * MOBA_BLOCK],
                             vbuf[slot, row * MOBA_TOPK + r].astype(BF16))
        r_s[row:row + 1, :] = acc
    inv = 1.0 / denom
    for h in range(N_HEADS):
        rs = slice(h * n_tok, (h + 1) * n_tok)
        hs = slice(h * D_HEAD, (h + 1) * D_HEAD)
        o_ref[:, hs] = ((r_s[rs, :] + o_own[rs, hs]) * inv[rs, :]).astype(o_ref.dtype)


def moba_sample_gather(idx, page_table, qbd, scores, k_new, v_new, slope_col, cache_v, past_len):
    n_seq, rows, _ = qbd.shape
    n_tok = rows // N_HEADS
    n_blocks = scores.shape[1]
    n_pages = page_table.shape[1]
    kern = functools.partial(_moba_gather_kernel, n_tok=n_tok, n_pages=n_pages, past_len=past_len)
    return pl.pallas_call(
        kern,
        grid_spec=pltpu.PrefetchScalarGridSpec(
            num_scalar_prefetch=2,
            grid=(n_seq,),
            in_specs=[pl.BlockSpec((None, rows, MIX_W), lambda b, i, p: (b, 0, 0)),
                      pl.BlockSpec((None, n_blocks, rows, MOBA_BLOCK), lambda b, i, p: (b, 0, 0, 0)),
                      pl.BlockSpec((None, n_tok, MIX_W), lambda b, i, p: (b, 0, 0)),
                      pl.BlockSpec((None, n_tok, MIX_W), lambda b, i, p: (b, 0, 0)),
                      pl.BlockSpec((rows, 1), lambda b, i, p: (0, 0)),
                      pl.BlockSpec(memory_space=pl.ANY)],
            out_specs=pl.BlockSpec((None, n_tok, MIX_W), lambda b, i, p: (b, 0, 0)),
            scratch_shapes=[pltpu.VMEM((2, rows * MOBA_TOPK, MOBA_BLOCK, D_HEAD), F32),
                            pltpu.VMEM((rows, MOBA_TOPK * MOBA_BLOCK), F32),
                            pltpu.VMEM((rows, MOBA_TOPK * MOBA_BLOCK), F32),
                            pltpu.VMEM((LANES, MIX_W), F32),
                            pltpu.VMEM((LANES, MIX_W), F32),
                            pltpu.VMEM((rows, D_HEAD), F32),
                            pltpu.SemaphoreType.DMA((2,))],
        ),
        out_shape=jax.ShapeDtypeStruct((n_seq, n_tok, MIX_W), F32),
        compiler_params=_params(1),
        name="moba_sample_gather",
    )(idx, page_table.reshape(-1), qbd, scores, k_new, v_new, slope_col, cache_v)


def _prep_weights(w_in, b_if, w_proj_m, w_proj_a, w_out, w_xq, w_xk, w_xv, w_xo, w_ffn_gate, w_ffn_up, w_ffn_down):
    n_gate = 2 * N_HEADS
    g0 = 4 * MIX_W
    w_main = jnp.concatenate([w_in[:, :g0], w_in[:, g0 + n_gate:]], axis=1).astype(BF16)
    return dict(w_main=w_main, w_if=w_in[:, g0:g0 + n_gate], b_if=b_if,
                w_proj_m=w_proj_m.astype(BF16), w_proj_a=w_proj_a.astype(BF16), w_out=w_out.astype(BF16),
                w_xq=w_xq.astype(BF16), w_xk=w_xk.astype(BF16), w_xv=w_xv.astype(BF16), w_xo=w_xo.astype(BF16),
                w_ffn_gate=w_ffn_gate.astype(BF16), w_ffn_up=w_ffn_up.astype(BF16),
                w_ffn_down=w_ffn_down.astype(BF16))


def _in_proj(x2d, norm_mix, W, tm):
    scale = D_HEAD ** -0.5
    xn = rmsnorm(x2d, norm_mix, tm)
    tn = 1024
    ones = jnp.ones((MIX_W,), F32)
    qkvo_scale = jnp.concatenate([ones, ones * scale, ones, ones])
    odt = BF16 if tm >= 16 and x2d.shape[0] > 128 else F32
    qkvo = matmul(xn, W["w_main"], 0, 4 * MIX_W, tm, tn, odt, qkvo_scale, name="in_proj_mlstm")
    gates = gate_preacts(xn, W["w_if"], W["b_if"], tm)
    q_a = matmul(xn, W["w_main"], 4 * MIX_W, MIX_W, tm, tn, BF16, ones * scale, name="in_proj_moba_q")
    k_a = matmul(xn, W["w_main"], 5 * MIX_W, MIX_W, tm, tn, F32, name="in_proj_moba_k")
    v_a = matmul(xn, W["w_main"], 6 * MIX_W, MIX_W, tm, tn, F32, name="in_proj_moba_v")
    g = matmul(xn, W["w_main"], 7 * MIX_W, 2 * D_MODEL, tm, tn, BF16, name="in_proj_branch_gates")
    return qkvo, gates, q_a, k_a, v_a, g


def _post_mixer(x2d, h_m, h_a, g, mem_k, mem_v, n_seq, W, norm_xattn, norm_ffn, norm_final, tm, tq):
    m = x2d.shape[0]
    u = gated_merge(h_m, h_a, W["w_proj_m"], W["w_proj_a"], g, tm, 1024)
    x1, xn2 = resid_matmul_norm(x2d, u, W["w_out"], norm_xattn, tm, name="out_proj_resid_norm")
    q_dtype = BF16 if tq >= 16 else F32
    xq = matmul(xn2, W["w_xq"], 0, XATTN_W, tm, XATTN_W, q_dtype,
                jnp.full((XATTN_W,), D_HEAD ** -0.5, F32), name="xattn_q_proj")
    xo = cross_attention(xq.reshape(n_seq, m // n_seq, XATTN_W), mem_k, mem_v, tq).reshape(m, XATTN_W)
    x2, xf = resid_matmul_norm(x1, xo.astype(BF16), W["w_xo"], norm_ffn, tm, name="xattn_o_proj_resid_norm")
    return ffn_final_norm(xf, x2, W["w_ffn_gate"], W["w_ffn_up"], W["w_ffn_down"], norm_final, tm, 512)


def kernel(x_prompt, x_sample, cache_k, cache_v, cache_mem_k, cache_mem_v, state_C, state_n, state_m, page_table, mem_prompt, norm_mix, w_in, b_if, mh_gain, w_proj_m, w_proj_a, w_out, norm_xattn, norm_mem, w_xq, w_xk, w_xv, w_xo, norm_ffn, w_ffn_gate, w_ffn_up, w_ffn_down, norm_final):
    n_layers = w_in.shape[0]
    assert n_layers == 1
    B, T, _ = x_prompt.shape
    DB, TS, _ = x_sample.shape
    n_pages = page_table.shape[1]
    past_len = n_pages * PAGE_SIZE
    assert past_len % MOBA_BLOCK == 0 and TS <= MOBA_BLOCK and T % MOBA_BLOCK == 0
    W = _prep_weights(w_in[0], b_if[0], w_proj_m[0], w_proj_a[0], w_out[0], w_xq[0], w_xk[0], w_xv[0], w_xo[0],
                      w_ffn_gate[0], w_ffn_up[0], w_ffn_down[0])
    slopes = jnp.exp2(-8.0 * jnp.arange(1, N_HEADS + 1, dtype=F32) / N_HEADS)

    TM_P = 512
    xp = x_prompt.reshape(B * T, D_MODEL)
    qkvo, gates, q_a, k_a, v_a, g = _in_proj(xp, norm_mix[0], W, 1024)
    chunk = 256
    h_m, c_p, n_p, m_p = mlstm(qkvo, gates, mh_gain[0], B, T // chunk, chunk, chunk, None, BF16)
    h_a = moba_prompt(q_a, k_a, v_a, slopes, B, T)
    mn = rmsnorm(mem_prompt.reshape(B * N_MEM, D_MODEL), norm_mem[0], 512)
    mk_p = matmul(mn, W["w_xk"], 0, XATTN_W, 512, XATTN_W, F32, name="mem_k_proj")
    mv_p = matmul(mn, W["w_xv"], 0, XATTN_W, 512, XATTN_W, F32, name="mem_v_proj")
    y_p = _post_mixer(xp, h_m, h_a, g, mk_p.reshape(B, N_MEM, XATTN_W), mv_p.reshape(B, N_MEM, XATTN_W), B, W,
                      norm_xattn[0], norm_ffn[0], norm_final, TM_P, 512)

    MS = DB * TS
    xs = x_sample.reshape(MS, D_MODEL)
    qkvo_s, gates_s, q_as, k_as, v_as, g_s = _in_proj(xs, norm_mix[0], W, MS)
    pad_t = 8
    pad_rows = lambda a: jnp.pad(a.reshape(DB, TS, -1), ((0, 0), (0, pad_t - TS), (0, 0))).reshape(DB * pad_t, -1)
    h_ms, c_s, n_s, m_s = mlstm(pad_rows(qkvo_s), pad_rows(gates_s), mh_gain[0], DB, 1, pad_t, TS,
                                (state_C[0], state_n[0], state_m[0]), F32)
    h_ms = h_ms.reshape(DB, pad_t, MIX_W)[:, :TS].reshape(MS, MIX_W).astype(BF16)

    q4 = q_as.reshape(DB, TS, N_HEADS, D_HEAD)
    eye = jnp.eye(N_HEADS, dtype=BF16)
    qbd = (q4.transpose(0, 2, 1, 3)[:, :, :, None, :] * eye[None, :, None, :, None]).reshape(DB, N_HEADS * TS, MIX_W)
    ck = cache_k.reshape(cache_k.shape[1:])
    cv = cache_v.reshape(cache_v.shape[1:])
    n_past_blocks = past_len // MOBA_BLOCK
    kbar, scores = moba_sample_scan(qbd, ck, page_table, DB, n_past_blocks)
    idx = moba_sample_pick(qbd, kbar)[:, :, :MOBA_TOPK].reshape(-1)
    slope_col = jnp.repeat(slopes, TS).reshape(N_HEADS * TS, 1)
    h_as = moba_sample_gather(idx, page_table, qbd, scores, k_as.reshape(DB, TS, MIX_W), v_as.reshape(DB, TS, MIX_W),
                              slope_col, cv, past_len)
    h_as = h_as.reshape(MS, MIX_W).astype(BF16)
    y_s = _post_mixer(xs, h_ms, h_as, g_s, cache_mem_k[0].reshape(DB, N_MEM, XATTN_W),
                      cache_mem_v[0].reshape(DB, N_MEM, XATTN_W), DB, W, norm_xattn[0], norm_ffn[0], norm_final, MS, TS)

    kv_shape_p = (1, B, T, N_HEADS, D_HEAD)
    kv_shape_s = (1, DB, TS, N_HEADS, D_HEAD)
    mem_shape = (1, B, N_MEM, XATTN_HEADS, D_HEAD)
    return (y_p.reshape(B, T, D_MODEL), y_s.reshape(DB, TS, D_MODEL),
            k_a.reshape(kv_shape_p), v_a.reshape(kv_shape_p),
            c_p[None].astype(state_C.dtype), n_p[None].astype(state_n.dtype), m_p[None].astype(state_m.dtype),
            mk_p.reshape(mem_shape), mv_p.reshape(mem_shape),
            k_as.reshape(kv_shape_s).astype(cache_k.dtype), v_as.reshape(kv_shape_s).astype(cache_v.dtype),
            c_s[None].astype(state_C.dtype), n_s[None].astype(state_n.dtype), m_s[None].astype(state_m.dtype))
```
